```python
import jax, jax.numpy as jnp
from jax import lax
import numpy as np

D_MODEL = 2048
BATCH = 2
SEQ = 16384
DEPTH = 1

D_MIX = D_MODEL
CONV_W = D_MIX // 2
CONV_GROUPS = 16
GLA_HEADS = 4
GLA_DV = D_MIX - CONV_W
HEAD_V = GLA_DV // GLA_HEADS
GLA_DK = GLA_DV // 2
HEAD_K = GLA_DK // GLA_HEADS
GATE_RANK = 16
GATE_TAU = 16
CHUNK = 64
CONV_K = 3
D_FF = 5632
EPS = 1e-6
D_IN = 3 * CONV_W + 2 * GLA_DK + 2 * GLA_DV + GATE_RANK

kernel_name = "hybrid_conv_gla_convffn_adaln"


def rmsnorm(x, w):
    x32 = x.astype(jnp.float32)
    y = x32 * lax.rsqrt(jnp.mean(x32 * x32, axis=-1, keepdims=True) + EPS)
    return (y * w.astype(jnp.float32)).astype(x.dtype)


def causal_dwconv3(u, w):
    s = u.shape[1]
    up = jnp.pad(u, ((0, 0), (CONV_K - 1, 0), (0, 0)))
    return up[:, :s] * w[0] + up[:, 1:s + 1] * w[1] + up[:, 2:] * w[2]


def gla_chunked(q, k, v, log_a):
    bsz, s, h, dk = q.shape
    dv = v.shape[-1]
    nc = s // CHUNK

    def to_chunks(t):
        return t.reshape(bsz, nc, CHUNK, h, t.shape[-1]).transpose(1, 0, 3, 2, 4).astype(jnp.float32)

    mask = jnp.tril(jnp.ones((CHUNK, CHUNK), dtype=bool))[None, None, :, :, None]

    def step(state, inp):
        qc, kc, vc, gc = inp
        b = jnp.cumsum(gc, axis=2)
        diff = b[:, :, :, None, :] - b[:, :, None, :, :]
        decay = jnp.exp(jnp.where(mask, diff, -jnp.inf))
        scores = jnp.einsum('bhid,bhjd,bhijd->bhij', qc, kc, decay)
        o = jnp.einsum('bhij,bhje->bhie', scores, vc) + jnp.einsum('bhid,bhde->bhie', qc * jnp.exp(b), state)
        b_last = b[:, :, -1:, :]
        state = jnp.exp(b_last[:, :, 0, :])[..., None] * state + jnp.einsum(
            'bhjd,bhje->bhde', kc * jnp.exp(b_last - b), vc)
        return state, o

    state0 = jnp.zeros((bsz, h, dk, dv), jnp.float32)
    _, o = lax.scan(step, state0, (to_chunks(q), to_chunks(k), to_chunks(v), to_chunks(log_a)))
    return o.transpose(1, 0, 3, 2, 4).reshape(bsz, s, h, dv)


def hybrid_mixer(h, w_in, conv_w, gate_w2, gate_b, gla_norm_w, w_out):
    bsz, s, _ = h.shape
    proj = h @ w_in
    splits = np.cumsum([CONV_W, CONV_W, CONV_W, GLA_DK, GLA_DK, GLA_DV, GLA_DV])
    cb, cc, cx, q, k, v, r, a_lr = jnp.split(proj, splits, axis=-1)
    y_conv = cb * causal_dwconv3(cc * cx, conv_w)
    log_a = jax.nn.log_sigmoid((a_lr @ gate_w2 + gate_b).astype(jnp.float32)) / GATE_TAU
    q = q.reshape(bsz, s, GLA_HEADS, HEAD_K) * (HEAD_K ** -0.5)
    k = k.reshape(bsz, s, GLA_HEADS, HEAD_K)
    v = v.reshape(bsz, s, GLA_HEADS, HEAD_V)
    log_a = log_a.reshape(bsz, s, GLA_HEADS, HEAD_K)
    o = gla_chunked(q, k, v, log_a).astype(h.dtype)
    o = rmsnorm(o, gla_norm_w) * jax.nn.silu(r.reshape(bsz, s, GLA_HEADS, HEAD_V))
    y_gla = o.reshape(bsz, s, GLA_DV)
    return jnp.concatenate([y_conv, y_gla], axis=-1) @ w_out


def conv_ffn(h, w_up, ffn_conv_w, w_down):
    u = causal_dwconv3(h @ w_up, ffn_conv_w)
    g, val = jnp.split(u, 2, axis=-1)
    return (jax.nn.silu(g) * val) @ w_down


def setup_inputs(seed: int = 0) -> dict:
    key = jax.random.key(seed)
    ks = jax.random.split(key, 20)
    n = jax.random.normal
    f = jnp.float32
    L = DEPTH
    return {
        "x": n(ks[0], (BATCH, SEQ, D_MODEL), f),
        "c": n(ks[1], (BATCH, D_MODEL), f),
        "w_mod": n(ks[2], (L, D_MODEL, 6 * D_MODEL), f) * (0.5 * D_MODEL ** -0.5),
        "b_mod": 0.01 * n(ks[3], (L, 6 * D_MODEL), f),
        "mix_pre_w": 1.0 + 0.05 * n(ks[4], (L, D_MODEL), f),
        "mix_post_w": 1.0 + 0.05 * n(ks[5], (L, D_MODEL), f),
        "w_in": n(ks[6], (L, D_MODEL, D_IN), f) * D_MODEL ** -0.5,
        "conv_w": n(ks[7], (L, CONV_K, CONV_W), f) * CONV_K ** -0.5,
        "gate_w2": n(ks[8], (L, GATE_RANK, GLA_DK), f) * GATE_RANK ** -0.5,
        "gate_b": 0.01 * n(ks[9], (L, GLA_DK), f),
        "gla_norm_w": 1.0 + 0.05 * n(ks[10], (L, HEAD_V), f),
        "w_out": n(ks[11], (L, D_MIX, D_MODEL), f) * D_MIX ** -0.5,
        "ffn_pre_w": 1.0 + 0.05 * n(ks[12], (L, D_MODEL), f),
        "ffn_post_w": 1.0 + 0.05 * n(ks[13], (L, D_MODEL), f),
        "w_up": n(ks[14], (L, D_MODEL, 2 * D_FF), f) * D_MODEL ** -0.5,
        "ffn_conv_w": n(ks[15], (L, CONV_K, 2 * D_FF), f) * CONV_K ** -0.5,
        "w_down": n(ks[16], (L, D_FF, D_MODEL), f) * D_FF ** -0.5,
    }


def reference(x, c, w_mod, b_mod, mix_pre_w, mix_post_w, w_in, conv_w, gate_w2, gate_b, gla_norm_w,
              w_out, ffn_pre_w, ffn_post_w, w_up, ffn_conv_w, w_down):
    c_act = jax.nn.silu(c)
    for l in range(DEPTH):
        mod = (c_act @ w_mod[l] + b_mod[l])[:, None, :]
        sh_m, sc_m, g_m, sh_f, sc_f, g_f = jnp.split(mod, 6, axis=-1)
        h = rmsnorm(x, mix_pre_w[l]) * (1 + sc_m) + sh_m
        y = hybrid_mixer(h, w_in[l], conv_w[l], gate_w2[l], gate_b[l], gla_norm_w[l], w_out[l])
        x = x + g_m * rmsnorm(y, mix_post_w[l])
        h = rmsnorm(x, ffn_pre_w[l]) * (1 + sc_f) + sh_f
        y = conv_ffn(h, w_up[l], ffn_conv_w[l], w_down[l])
        x = x + g_f * rmsnorm(y, ffn_post_w[l])
    return x
```

```python
import functools

import jax
import jax.numpy as jnp
from jax import lax
from jax.experimental import pallas as pl
from jax.experimental.pallas import tpu as pltpu

F32 = jnp.float32
BF16 = jnp.bfloat16

D_MODEL = 2048
CONV_W = 1024
GLA_HEADS = 4
HEAD_K = 128
HEAD_V = 256
GLA_DK = GLA_HEADS * HEAD_K
GLA_DV = GLA_HEADS * HEAD_V
GATE_RANK = 16
GATE_TAU = 16
D_FF = 5632
EPS = 1e-6
D_IN = 3 * CONV_W + 2 * GLA_DK + 2 * GLA_DV + GATE_RANK

LANES = 128
SUBLANES = 8
D_MAIN = D_IN - GATE_RANK
D_PROJ = D_MAIN + LANES
SUB = 32
VMEM_LIMIT = 56 * 1024 * 1024

TM_IN = 1024
TN_IN = 896
TM_MIX = 256
TM_OUT = 512
TM_FFN = 512
TF_FFN = 512


def _rms_scale(x):
    return lax.rsqrt(jnp.mean(x * x, axis=-1, keepdims=True) + EPS)


def _sigmoid(x):
    return 1.0 / (1.0 + jnp.exp(-x))


def _mod_kernel(c_ref, w_ref, b_ref, o_ref):
    c = c_ref[...]
    c_act = c * _sigmoid(c)
    o_ref[...] = jnp.dot(c_act, w_ref[...], preferred_element_type=F32) + b_ref[...]


def _modulation(c_pad, w_mod, b_mod):
    n = w_mod.shape[1]
    tn = 1024
    return pl.pallas_call(
        _mod_kernel,
        grid=(n // tn,),
        in_specs=[
            pl.BlockSpec((SUBLANES, D_MODEL), lambda j: (0, 0)),
            pl.BlockSpec((D_MODEL, tn), lambda j: (0, j)),
            pl.BlockSpec((1, tn), lambda j: (0, j)),
        ],
        out_specs=pl.BlockSpec((SUBLANES, tn), lambda j: (0, j)),
        out_shape=jax.ShapeDtypeStruct((SUBLANES, n), F32),
        compiler_params=pltpu.CompilerParams(
            dimension_semantics=("arbitrary",), vmem_limit_bytes=VMEM_LIMIT),
        name="modulation",
    )(c_pad, w_mod, b_mod)


def _inproj_kernel(x_ref, nw_ref, sc_ref, sh_ref, w_ref, o_ref, h_ref):
    @pl.when(pl.program_id(2) == 0)
    def _():
        x = x_ref[0]
        y = x * _rms_scale(x) * nw_ref[...]
        h_ref[...] = (y * (1.0 + sc_ref[0]) + sh_ref[0]).astype(BF16)

    o_ref[0] = jnp.dot(h_ref[...], w_ref[...], preferred_element_type=F32).astype(o_ref.dtype)


def _in_proj(x, norm_w, scale, shift, w_in_pad):
    bsz, seq, _ = x.shape
    tm = min(TM_IN, seq)
    return pl.pallas_call(
        _inproj_kernel,
        grid=(bsz, seq // tm, D_PROJ // TN_IN),
        in_specs=[
            pl.BlockSpec((1, tm, D_MODEL), lambda b, i, j: (b, i, 0)),
            pl.BlockSpec((1, D_MODEL), lambda b, i, j: (0, 0)),
            pl.BlockSpec((1, 1, D_MODEL), lambda b, i, j: (b, 0, 0)),
            pl.BlockSpec((1, 1, D_MODEL), lambda b, i, j: (b, 0, 0)),
            pl.BlockSpec((D_MODEL, TN_IN), lambda b, i, j: (0, j)),
        ],
        out_specs=pl.BlockSpec((1, tm, TN_IN), lambda b, i, j: (b, i, j)),
        out_shape=jax.ShapeDtypeStruct((bsz, seq, D_PROJ), BF16),
        scratch_shapes=[pltpu.VMEM((tm, D_MODEL), BF16)],
        compiler_params=pltpu.CompilerParams(
            dimension_semantics=("arbitrary", "arbitrary", "arbitrary"),
            vmem_limit_bytes=VMEM_LIMIT),
        name="in_proj",
    )(x, norm_w, scale, shift, w_in_pad)


def _gla_subchunk(qk_ref, v_ref, r_ref, b_scr, s_scr, gnw, o_ref, rows):
    scale = HEAD_K ** -0.5
    lane = lax.broadcasted_iota(jnp.int32, (SUB, SUB), 1)
    for h in range(GLA_HEADS):
        ks = slice(h * HEAD_K, (h + 1) * HEAD_K)
        vs = slice(h * HEAD_V, (h + 1) * HEAD_V)
        bq = b_scr[rows, ks]
        q = qk_ref[0, rows, ks].astype(F32) * scale
        k = qk_ref[0, rows, GLA_DK + h * HEAD_K:GLA_DK + (h + 1) * HEAD_K].astype(F32)
        v = v_ref[0, rows, vs]
        b_last = bq[SUB - 1:SUB, :]

        p = jnp.zeros((SUB, SUB), F32)
        for j in range(SUB):
            g0 = (j // SUBLANES) * SUBLANES
            d = bq[g0:, :] - bq[j:j + 1, :]
            rid = lax.broadcasted_iota(jnp.int32, (SUBLANES, 1), 0) + g0
            d_head = jnp.where(rid >= j, d[:SUBLANES], -jnp.inf)
            d = d_head if g0 + SUBLANES == SUB else jnp.concatenate([d_head, d[SUBLANES:]], axis=0)
            col = jnp.sum(q[g0:, :] * k[j:j + 1, :] * jnp.exp(d), axis=-1, keepdims=True)
            if g0:
                col = jnp.concatenate([jnp.zeros((g0, 1), F32), col], axis=0)
            p = jnp.where(lane == j, col, p)

        s_old = s_scr[h]
        qe = (q * jnp.exp(bq)).astype(BF16)
        o = jnp.dot(qe, s_old.astype(BF16), preferred_element_type=F32)
        o = o + jnp.dot(p.astype(BF16), v, preferred_element_type=F32)

        ke = (k * jnp.exp(b_last - bq)).astype(BF16)
        upd = lax.dot_general(ke, v, (((0,), (0,)), ((), ())), preferred_element_type=F32)
        decay = jnp.transpose(jnp.broadcast_to(jnp.exp(b_last), (LANES, HEAD_K)))[:, 0:1]
        s_scr[h] = decay * s_old + upd

        r = r_ref[0, rows, vs].astype(F32)
        y = (o * _rms_scale(o) * gnw) * (r * _sigmoid(r))
        o_ref[0, rows, CONV_W + h * HEAD_V:CONV_W + (h + 1) * HEAD_V] = y.astype(o_ref.dtype)


def _mixer_kernel(cb_ref, cc_ref, cx_ref, qk_ref, v_ref, r_ref, alr_ref, ltri_ref, convw_ref,
                  gw2_ref, gb_ref, gnw_ref, o_ref, u_scr, b_scr, s_scr):
    i = pl.program_id(1)
    tm = cb_ref.shape[1]

    @pl.when(i == 0)
    def _():
        u_scr[0:SUBLANES, :] = jnp.zeros((SUBLANES, CONV_W), F32)
        s_scr[...] = jnp.zeros(s_scr.shape, F32)

    @pl.when(i > 0)
    def _():
        u_scr[0:SUBLANES, :] = u_scr[tm:tm + SUBLANES, :]

    u = cc_ref[0].astype(F32) * cx_ref[0].astype(F32)
    u_scr[SUBLANES:tm + SUBLANES, :] = u
    cw = convw_ref[...]
    y = (u_scr[SUBLANES - 2:tm + SUBLANES - 2, :] * cw[0:1]
         + u_scr[SUBLANES - 1:tm + SUBLANES - 1, :] * cw[1:2]
         + u * cw[2:3])
    o_ref[0, :, 0:CONV_W] = (cb_ref[0].astype(F32) * y).astype(o_ref.dtype)

    z = jnp.dot(alr_ref[0], gw2_ref[...], preferred_element_type=F32) + gb_ref[...]
    log_a = (jnp.minimum(z, 0.0) - jnp.log1p(jnp.exp(-jnp.abs(z)))) * (1.0 / GATE_TAU)
    hi = log_a.astype(BF16)
    rem = log_a - hi.astype(F32)
    mid = rem.astype(BF16)
    lo = (rem - mid.astype(F32)).astype(BF16)
    ltri = ltri_ref[...]
    b_scr[...] = (jnp.dot(ltri, hi, preferred_element_type=F32)
                  + jnp.dot(ltri, mid, preferred_element_type=F32)
                  + jnp.dot(ltri, lo, preferred_element_type=F32))

    gnw = gnw_ref[...]

    def body(c, carry):
        rows = pl.ds(pl.multiple_of(c * SUB, SUB), SUB)
        _gla_subchunk(qk_ref, v_ref, r_ref, b_scr, s_scr, gnw, o_ref, rows)
        return carry

    lax.fori_loop(0, tm // SUB, body, 0)


def _mixer(proj, ltri, conv_w, gw2_pad, gate_b, gla_norm_w):
    bsz, seq, _ = proj.shape
    tm = min(TM_MIX, seq)
    wide = lambda col: pl.BlockSpec((1, tm, CONV_W), lambda b, i, col=col: (b, i, col))
    const = lambda shape: pl.BlockSpec(shape, lambda b, i: (0,) * len(shape))
    return pl.pallas_call(
        _mixer_kernel,
        grid=(bsz, seq // tm),
        in_specs=[
            wide(0), wide(1), wide(2), wide(3), wide(4), wide(5),
            pl.BlockSpec((1, tm, LANES), lambda b, i: (b, i, D_MAIN // LANES)),
            const((tm, tm)), const((3, CONV_W)), const((LANES, GLA_DK)), const((1, GLA_DK)),
            const((1, HEAD_V)),
        ],
        out_specs=pl.BlockSpec((1, tm, D_MODEL), lambda b, i: (b, i, 0)),
        out_shape=jax.ShapeDtypeStruct((bsz, seq, D_MODEL), BF16),
        scratch_shapes=[
            pltpu.VMEM((tm + SUBLANES, CONV_W), F32),
            pltpu.VMEM((tm, GLA_DK), F32),
            pltpu.VMEM((GLA_HEADS, HEAD_K, HEAD_V), F32),
        ],
        compiler_params=pltpu.CompilerParams(
            dimension_semantics=("arbitrary", "arbitrary"), vmem_limit_bytes=VMEM_LIMIT),
        name="mixer",
    )(proj, proj, proj, proj, proj, proj, proj, ltri, conv_w, gw2_pad, gate_b, gla_norm_w)


def _outproj_kernel(y_ref, x_ref, w_ref, nw_ref, g_ref, o_ref):
    y = jnp.dot(y_ref[0], w_ref[...], preferred_element_type=F32)
    o_ref[0] = x_ref[0] + g_ref[0] * (y * _rms_scale(y) * nw_ref[...])


def _out_proj(y_mix, x, w_out, norm_w, gate):
    bsz, seq, _ = x.shape
    tm = min(TM_OUT, seq)
    return pl.pallas_call(
        _outproj_kernel,
        grid=(bsz, seq // tm),
        in_specs=[
            pl.BlockSpec((1, tm, D_MODEL), lambda b, i: (b, i, 0)),
            pl.BlockSpec((1, tm, D_MODEL), lambda b, i: (b, i, 0)),
            pl.BlockSpec((D_MODEL, D_MODEL), lambda b, i: (0, 0)),
            pl.BlockSpec((1, D_MODEL), lambda b, i: (0, 0)),
            pl.BlockSpec((1, 1, D_MODEL), lambda b, i: (b, 0, 0)),
        ],
        out_specs=pl.BlockSpec((1, tm, D_MODEL), lambda b, i: (b, i, 0)),
        out_shape=jax.ShapeDtypeStruct((bsz, seq, D_MODEL), F32),
        compiler_params=pltpu.CompilerParams(
            dimension_semantics=("arbitrary", "arbitrary"), vmem_limit_bytes=VMEM_LIMIT),
        name="out_proj",
    )(y_mix, x, w_out, norm_w, gate)


def _conv3_rows(u, u_scr, hist_ref, cw, first_tile):
    tm = u.shape[0]

    @pl.when(first_tile)
    def _():
        u_scr[0:SUBLANES, :] = jnp.zeros((SUBLANES, u.shape[1]), F32)

    @pl.when(jnp.logical_not(first_tile))
    def _():
        u_scr[0:SUBLANES, :] = hist_ref[0]

    u_scr[SUBLANES:tm + SUBLANES, :] = u
    hist_ref[0] = u[tm - SUBLANES:, :]
    return (u_scr[SUBLANES - 2:tm + SUBLANES - 2, :] * cw[0:1]
            + u_scr[SUBLANES - 1:tm + SUBLANES - 1, :] * cw[1:2]
            + u * cw[2:3])


def _ffn_kernel(x_ref, nw_ref, sc_ref, sh_ref, wg_ref, wv_ref, cwg_ref, cwv_ref, wd_ref,
                pw_ref, g_ref, o_ref, h_scr, acc_scr, ug_scr, uv_scr, hg_scr, hv_scr):
    i = pl.program_id(1)
    f = pl.program_id(2)
    nf = pl.num_programs(2)

    @pl.when(f == 0)
    def _():
        x = x_ref[0]
        y = x * _rms_scale(x) * nw_ref[...]
        h_scr[...] = (y * (1.0 + sc_ref[0]) + sh_ref[0]).astype(BF16)
        acc_scr[...] = jnp.zeros(acc_scr.shape, F32)

    h = h_scr[...]
    ug = jnp.dot(h, wg_ref[...], preferred_element_type=F32)
    uv = jnp.dot(h, wv_ref[...], preferred_element_type=F32)
    cg = _conv3_rows(ug, ug_scr, hg_scr.at[pl.ds(f, 1)], cwg_ref[...], i == 0)
    cv = _conv3_rows(uv, uv_scr, hv_scr.at[pl.ds(f, 1)], cwv_ref[...], i == 0)
    act = (cg * _sigmoid(cg) * cv).astype(BF16)
    acc_scr[...] += jnp.dot(act, wd_ref[...], preferred_element_type=F32)

    @pl.when(f == nf - 1)
    def _():
        y = acc_scr[...]
        o_ref[0] = x_ref[0] + g_ref[0] * (y * _rms_scale(y) * pw_ref[...])


def _conv_ffn(x1, norm_w, scale, shift, w_up, ffn_conv_w, w_down, post_w, gate):
    bsz, seq, _ = x1.shape
    tm = min(TM_FFN, seq)
    tf = TF_FFN
    nf = D_FF // tf
    row = lambda shape: pl.BlockSpec(shape, lambda b, i, f: (0, 0))
    per_b = pl.BlockSpec((1, 1, D_MODEL), lambda b, i, f: (b, 0, 0))
    return pl.pallas_call(
        _ffn_kernel,
        grid=(bsz, seq // tm, nf),
        in_specs=[
            pl.BlockSpec((1, tm, D_MODEL), lambda b, i, f: (b, i, 0)),
            row((1, D_MODEL)), per_b, per_b,
            pl.BlockSpec((D_MODEL, tf), lambda b, i, f: (0, f)),
            pl.BlockSpec((D_MODEL, tf), lambda b, i, f: (0, f + nf)),
            pl.BlockSpec((3, tf), lambda b, i, f: (0, f)),
            pl.BlockSpec((3, tf), lambda b, i, f: (0, f + nf)),
            pl.BlockSpec((tf, D_MODEL), lambda b, i, f: (f, 0)),
            row((1, D_MODEL)), per_b,
        ],
        out_specs=pl.BlockSpec((1, tm, D_MODEL), lambda b, i, f: (b, i, 0)),
        out_shape=jax.ShapeDtypeStruct((bsz, seq, D_MODEL), F32),
        scratch_shapes=[
            pltpu.VMEM((tm, D_MODEL), BF16),
            pltpu.VMEM((tm, D_MODEL), F32),
            pltpu.VMEM((tm + SUBLANES, tf), F32),
            pltpu.VMEM((tm + SUBLANES, tf), F32),
            pltpu.VMEM((nf, SUBLANES, tf), F32),
            pltpu.VMEM((nf, SUBLANES, tf), F32),
        ],
        compiler_params=pltpu.CompilerParams(
            dimension_semantics=("arbitrary", "arbitrary", "arbitrary"),
            vmem_limit_bytes=VMEM_LIMIT),
        name="conv_ffn",
    )(x1, norm_w, scale, shift, w_up, w_up, ffn_conv_w, ffn_conv_w, w_down, post_w, gate)


def _block_tril(n, blk):
    r = lax.broadcasted_iota(jnp.int32, (n, n), 0)
    c = lax.broadcasted_iota(jnp.int32, (n, n), 1)
    return ((c <= r) & (c // blk == r // blk)).astype(BF16)


def kernel(x, c, w_mod, b_mod, mix_pre_w, mix_post_w, w_in, conv_w, gate_w2, gate_b, gla_norm_w,
           w_out, ffn_pre_w, ffn_post_w, w_up, ffn_conv_w, w_down):
    bsz, seq, _ = x.shape
    depth = w_mod.shape[0]
    c_pad = jnp.zeros((SUBLANES, D_MODEL), F32).at[:bsz].set(c)
    ltri = _block_tril(min(TM_MIX, seq), SUB)
    for l in range(depth):
        mod = _modulation(c_pad, w_mod[l], b_mod[l][None, :])[:bsz]
        sh_m, sc_m, g_m, sh_f, sc_f, g_f = [m[:, None, :] for m in jnp.split(mod, 6, axis=-1)]

        w_in_pad = jnp.pad(w_in[l].astype(BF16), ((0, 0), (0, D_PROJ - D_IN)))
        gw2_pad = jnp.pad(gate_w2[l].astype(BF16), ((0, LANES - GATE_RANK), (0, 0)))
        proj = _in_proj(x, mix_pre_w[l][None, :], sc_m, sh_m, w_in_pad)
        y_mix = _mixer(proj, ltri, conv_w[l], gw2_pad, gate_b[l][None, :], gla_norm_w[l][None, :])
        x = _out_proj(y_mix, x, w_out[l].astype(BF16), mix_post_w[l][None, :], g_m)
        x = _conv_ffn(x, ffn_pre_w[l][None, :], sc_f, sh_f, w_up[l].astype(BF16), ffn_conv_w[l],
                      w_down[l].astype(BF16), ffn_post_w[l][None, :], g_f)
    return x
```

```python
import jax
import jax.numpy as jnp
from jax import lax
from jax.experimental import pallas as pl
from jax.experimental.pallas import tpu as pltpu

F32 = jnp.float32
BF16 = jnp.bfloat16

D_MODEL = 2048
CONV_W = 1024
GLA_HEADS = 4
HEAD_K = 128
HEAD_V = 256
GLA_DK = GLA_HEADS * HEAD_K
GLA_DV = GLA_HEADS * HEAD_V
GATE_RANK = 16
GATE_TAU = 16
D_FF = 5632
EPS = 1e-6
D_IN = 3 * CONV_W + 2 * GLA_DK + 2 * GLA_DV + GATE_RANK

LANES = 128
SUBLANES = 8
D_MAIN = D_IN - GATE_RANK
D_PROJ = D_MAIN + LANES
CHUNK = 256
LEVELS = (128, 64, 32, 16, 8)
VMEM_LIMIT = 56 * 1024 * 1024

TM_IN = 1024
TN_IN = 896
TM_OUT = 512
TM_FFN = 512
TF_FFN = 512


def _rms_scale(x):
    return lax.rsqrt(jnp.mean(x * x, axis=-1, keepdims=True) + EPS)


def _sigmoid(x):
    return 1.0 / (1.0 + jnp.exp(-x))


def _mod_kernel(c_ref, w_ref, b_ref, o_ref):
    c = c_ref[...]
    c_act = c * _sigmoid(c)
    o_ref[...] = jnp.dot(c_act, w_ref[...], preferred_element_type=F32) + b_ref[...]


def _modulation(c_pad, w_mod, b_mod):
    n = w_mod.shape[1]
    tn = 1024
    return pl.pallas_call(
        _mod_kernel,
        grid=(n // tn,),
        in_specs=[
            pl.BlockSpec((SUBLANES, D_MODEL), lambda j: (0, 0)),
            pl.BlockSpec((D_MODEL, tn), lambda j: (0, j)),
            pl.BlockSpec((1, tn), lambda j: (0, j)),
        ],
        out_specs=pl.BlockSpec((SUBLANES, tn), lambda j: (0, j)),
        out_shape=jax.ShapeDtypeStruct((SUBLANES, n), F32),
        compiler_params=pltpu.CompilerParams(
            dimension_semantics=("arbitrary",), vmem_limit_bytes=VMEM_LIMIT),
        name="modulation",
    )(c_pad, w_mod, b_mod)


def _inproj_kernel(x_ref, nw_ref, sc_ref, sh_ref, w_ref, o_ref, h_ref):
    @pl.when(pl.program_id(2) == 0)
    def _():
        x = x_ref[0]
        y = x * _rms_scale(x) * nw_ref[...]
        h_ref[...] = (y * (1.0 + sc_ref[0]) + sh_ref[0]).astype(BF16)

    o_ref[0] = jnp.dot(h_ref[...], w_ref[...], preferred_element_type=F32).astype(o_ref.dtype)


def _in_proj(x, norm_w, scale, shift, w_in_pad):
    bsz, seq, _ = x.shape
    tm = min(TM_IN, seq)
    return pl.pallas_call(
        _inproj_kernel,
        grid=(bsz, seq // tm, D_PROJ // TN_IN),
        in_specs=[
            pl.BlockSpec((1, tm, D_MODEL), lambda b, i, j: (b, i, 0)),
            pl.BlockSpec((1, D_MODEL), lambda b, i, j: (0, 0)),
            pl.BlockSpec((1, 1, D_MODEL), lambda b, i, j: (b, 0, 0)),
            pl.BlockSpec((1, 1, D_MODEL), lambda b, i, j: (b, 0, 0)),
            pl.BlockSpec((D_MODEL, TN_IN), lambda b, i, j: (0, j)),
        ],
        out_specs=pl.BlockSpec((1, tm, TN_IN), lambda b, i, j: (b, i, j)),
        out_shape=jax.ShapeDtypeStruct((bsz, seq, D_PROJ), BF16),
        scratch_shapes=[pltpu.VMEM((tm, D_MODEL), BF16)],
        compiler_params=pltpu.CompilerParams(
            dimension_semantics=("arbitrary", "arbitrary", "arbitrary"),
            vmem_limit_bytes=VMEM_LIMIT),
        name="in_proj",
    )(x, norm_w, scale, shift, w_in_pad)


_NT = (((1,), (1,)), ((), ()))
_TN = (((0,), (0,)), ((), ()))
LOG2E = 1.4426950408889634


def _rep_rows(x, row, group):
    n, w = x.shape
    xr = x.reshape(n // group, group, w)
    return jnp.broadcast_to(xr[:, row:row + 1, :], xr.shape).reshape(n, w)


def _rep_rows_ref(ref, row, cols):
    n = ref.shape[0]
    return jnp.concatenate(
        [jnp.broadcast_to(ref[g + row:g + row + 1, cols], (SUBLANES, cols.stop - cols.start))
         for g in range(0, n, SUBLANES)], axis=0)


def _gla_head(h, qk_ref, v_ref, r_ref, b_scr, k_scr, st_scr, lvl, dlane, gnw, o_ref):
    ks = slice(h * HEAD_K, (h + 1) * HEAD_K)
    vs = slice(h * HEAD_V, (h + 1) * HEAD_V)
    half = CHUNK // 2
    b = b_scr[:, ks]
    q = qk_ref[0, :, ks].astype(F32) * (HEAD_K ** -0.5)
    k = qk_ref[0, :, GLA_DK + h * HEAD_K:GLA_DK + (h + 1) * HEAD_K].astype(F32)
    k_scr[...] = k
    v = v_ref[0, :, vs]
    row = lax.broadcasted_iota(jnp.int32, (CHUNK, 1), 0)

    p = jnp.zeros((CHUNK, CHUNK), F32)
    for idx, s in enumerate(LEVELS):
        b_ref = _rep_rows(b, s - 1, 2 * s)
        e = jnp.exp2(jnp.where((row & s) != 0, b - b_ref, b_ref - b))
        out = lax.dot_general((q * e).astype(BF16), (k * e).astype(BF16), _NT,
                              preferred_element_type=F32)
        p = jnp.where(lvl == idx, out, p)

    p_top = p[:half, :half]
    p_bot = p[half:, half:]
    row_in_block = row & (SUBLANES - 1)
    for j in range(SUBLANES):
        d = jnp.where(row_in_block >= j, b - _rep_rows_ref(b_scr, j, ks), -jnp.inf)
        col = jnp.sum(q * _rep_rows_ref(k_scr, j, slice(0, HEAD_K)) * jnp.exp2(d),
                      axis=-1, keepdims=True)
        p_top = jnp.where(dlane == j, col[:half], p_top)
        p_bot = jnp.where(dlane == j, col[half:], p_bot)
    p = jnp.concatenate(
        [jnp.concatenate([p_top, jnp.zeros((half, half), F32)], axis=1),
         jnp.concatenate([p[half:, :half], p_bot], axis=1)], axis=0)

    st = st_scr[h]
    b_last = b[CHUNK - 1:CHUNK, :]
    qe = (q * jnp.exp2(b)).astype(BF16)
    o = jnp.dot(p.astype(BF16), v, preferred_element_type=F32)
    o = o + lax.dot_general(qe, st.astype(BF16), _NT, preferred_element_type=F32)

    ke = (k * jnp.exp2(b_last - b)).astype(BF16)
    st_scr[h] = st * jnp.exp2(b_last) + lax.dot_general(v, ke, _TN, preferred_element_type=F32)

    r = r_ref[0, :, vs].astype(F32)
    y = (o * _rms_scale(o) * gnw) * (r * _sigmoid(r))
    o_ref[0, :, vs] = y.astype(o_ref.dtype)


def _gla_kernel(qk_ref, v_ref, r_ref, alr_ref, ltri_ref, lvl_ref, gw2_ref, gb_ref, gnw_ref,
                o_ref, b_scr, k_scr, st_scr):
    @pl.when(pl.program_id(1) == 0)
    def _():
        st_scr[...] = jnp.zeros(st_scr.shape, F32)

    z = jnp.dot(alr_ref[0], gw2_ref[...], preferred_element_type=F32) + gb_ref[...]
    log_a = (jnp.minimum(z, 0.0) - jnp.log(1.0 + jnp.exp(-jnp.abs(z)))) * (1.0 / GATE_TAU)
    hi = log_a.astype(BF16)
    rem = log_a - hi.astype(F32)
    mid = rem.astype(BF16)
    lo = (rem - mid.astype(F32)).astype(BF16)
    ltri = ltri_ref[...]
    b_scr[...] = LOG2E * (jnp.dot(ltri, hi, preferred_element_type=F32)
                          + jnp.dot(ltri, mid, preferred_element_type=F32)
                          + jnp.dot(ltri, lo, preferred_element_type=F32))

    half = CHUNK // 2
    dlane = (lax.broadcasted_iota(jnp.int32, (half, half), 1)
             - (lax.broadcasted_iota(jnp.int32, (half, half), 0) & ~(SUBLANES - 1)))
    gnw = gnw_ref[...]
    lvl = lvl_ref[...]
    for h in range(GLA_HEADS):
        _gla_head(h, qk_ref, v_ref, r_ref, b_scr, k_scr, st_scr, lvl, dlane, gnw, o_ref)


def _gla(proj, ltri, lvl, gw2_pad, gate_b, gla_norm_w):
    bsz, seq, _ = proj.shape
    assert seq % CHUNK == 0
    wide = lambda col: pl.BlockSpec((1, CHUNK, GLA_DV), lambda b, i, col=col: (b, i, col))
    const = lambda shape: pl.BlockSpec(shape, lambda b, i: (0,) * len(shape))
    return pl.pallas_call(
        _gla_kernel,
        grid=(bsz, seq // CHUNK),
        in_specs=[
            wide(3), wide(4), wide(5),
            pl.BlockSpec((1, CHUNK, LANES), lambda b, i: (b, i, D_MAIN // LANES)),
            const((CHUNK, CHUNK)), const((CHUNK, CHUNK)), const((LANES, GLA_DK)),
            const((1, GLA_DK)), const((1, HEAD_V)),
        ],
        out_specs=pl.BlockSpec((1, CHUNK, GLA_DV), lambda b, i: (b, i, 0)),
        out_shape=jax.ShapeDtypeStruct((bsz, seq, GLA_DV), BF16),
        scratch_shapes=[
            pltpu.VMEM((CHUNK, GLA_DK), F32),
            pltpu.VMEM((CHUNK, HEAD_K), F32),
            pltpu.VMEM((GLA_HEADS, HEAD_V, HEAD_K), F32),
        ],
        compiler_params=pltpu.CompilerParams(
            dimension_semantics=("arbitrary", "arbitrary"), vmem_limit_bytes=VMEM_LIMIT),
        name="gla",
    )(proj, proj, proj, proj, ltri, lvl, gw2_pad, gate_b, gla_norm_w)


def _outproj_kernel(cb_ref, cc_ref, cx_ref, yg_ref, x_ref, cw_ref, wc_ref, wg_ref, nw_ref, g_ref,
                    o_ref, u_scr):
    tm = cb_ref.shape[1]

    @pl.when(pl.program_id(1) == 0)
    def _():
        u_scr[0:SUBLANES, :] = jnp.zeros((SUBLANES, CONV_W), F32)

    @pl.when(pl.program_id(1) > 0)
    def _():
        u_scr[0:SUBLANES, :] = u_scr[tm:tm + SUBLANES, :]

    u = cc_ref[0].astype(F32) * cx_ref[0].astype(F32)
    u_scr[SUBLANES:tm + SUBLANES, :] = u
    cw = cw_ref[...]
    conv = (u_scr[SUBLANES - 2:tm + SUBLANES - 2, :] * cw[0:1]
            + u_scr[SUBLANES - 1:tm + SUBLANES - 1, :] * cw[1:2]
            + u * cw[2:3])
    y_conv = (cb_ref[0].astype(F32) * conv).astype(BF16)

    y = (jnp.dot(y_conv, wc_ref[...], preferred_element_type=F32)
         + jnp.dot(yg_ref[0], wg_ref[...], preferred_element_type=F32))
    o_ref[0] = x_ref[0] + g_ref[0] * (y * _rms_scale(y) * nw_ref[...])


def _out_proj(proj, y_gla, x, conv_w, w_out, norm_w, gate):
    bsz, seq, _ = x.shape
    tm = min(TM_OUT, seq)
    wide = lambda col: pl.BlockSpec((1, tm, CONV_W), lambda b, i, col=col: (b, i, col))
    const = lambda shape: pl.BlockSpec(shape, lambda b, i: (0,) * len(shape))
    return pl.pallas_call(
        _outproj_kernel,
        grid=(bsz, seq // tm),
        in_specs=[
            wide(0), wide(1), wide(2),
            pl.BlockSpec((1, tm, GLA_DV), lambda b, i: (b, i, 0)),
            pl.BlockSpec((1, tm, D_MODEL), lambda b, i: (b, i, 0)),
            const((3, CONV_W)),
            pl.BlockSpec((CONV_W, D_MODEL), lambda b, i: (0, 0)),
            pl.BlockSpec((GLA_DV, D_MODEL), lambda b, i: (1, 0)),
            const((1, D_MODEL)),
            pl.BlockSpec((1, 1, D_MODEL), lambda b, i: (b, 0, 0)),
        ],
        out_specs=pl.BlockSpec((1, tm, D_MODEL), lambda b, i: (b, i, 0)),
        out_shape=jax.ShapeDtypeStruct((bsz, seq, D_MODEL), F32),
        scratch_shapes=[pltpu.VMEM((tm + SUBLANES, CONV_W), F32)],
        compiler_params=pltpu.CompilerParams(
            dimension_semantics=("arbitrary", "arbitrary"), vmem_limit_bytes=VMEM_LIMIT),
        name="out_proj",
    )(proj, proj, proj, y_gla, x, conv_w, w_out, w_out, norm_w, gate)


def _conv3_rows(u, u_scr, hist_ref, cw, first_tile):
    tm = u.shape[0]

    @pl.when(first_tile)
    def _():
        u_scr[0:SUBLANES, :] = jnp.zeros((SUBLANES, u.shape[1]), F32)

    @pl.when(jnp.logical_not(first_tile))
    def _():
        u_scr[0:SUBLANES, :] = hist_ref[0]

    u_scr[SUBLANES:tm + SUBLANES, :] = u
    hist_ref[0] = u[tm - SUBLANES:, :]
    return (u_scr[SUBLANES - 2:tm + SUBLANES - 2, :] * cw[0:1]
            + u_scr[SUBLANES - 1:tm + SUBLANES - 1, :] * cw[1:2]
            + u * cw[2:3])


def _ffn_kernel(x_ref, nw_ref, sc_ref, sh_ref, wg_ref, wv_ref, cwg_ref, cwv_ref, wd_ref,
                pw_ref, g_ref, o_ref, h_scr, acc_scr, ug_scr, uv_scr, hg_scr, hv_scr):
    i = pl.program_id(1)
    f = pl.program_id(2)
    nf = pl.num_programs(2)

    @pl.when(f == 0)
    def _():
        x = x_ref[0]
        y = x * _rms_scale(x) * nw_ref[...]
        h_scr[...] = (y * (1.0 + sc_ref[0]) + sh_ref[0]).astype(BF16)
        acc_scr[...] = jnp.zeros(acc_scr.shape, F32)

    h = h_scr[...]
    ug = jnp.dot(h, wg_ref[...], preferred_element_type=F32)
    uv = jnp.dot(h, wv_ref[...], preferred_element_type=F32)
    cg = _conv3_rows(ug, ug_scr, hg_scr.at[pl.ds(f, 1)], cwg_ref[...], i == 0)
    cv = _conv3_rows(uv, uv_scr, hv_scr.at[pl.ds(f, 1)], cwv_ref[...], i == 0)
    act = (cg * _sigmoid(cg) * cv).astype(BF16)
    acc_scr[...] += jnp.dot(act, wd_ref[...], preferred_element_type=F32)

    @pl.when(f == nf - 1)
    def _():
        y = acc_scr[...]
        o_ref[0] = x_ref[0] + g_ref[0] * (y * _rms_scale(y) * pw_ref[...])


def _conv_ffn(x1, norm_w, scale, shift, w_up, ffn_conv_w, w_down, post_w, gate):
    bsz, seq, _ = x1.shape
    tm = min(TM_FFN, seq)
    tf = TF_FFN
    nf = D_FF // tf
    row = lambda shape: pl.BlockSpec(shape, lambda b, i, f: (0, 0))
    per_b = pl.BlockSpec((1, 1, D_MODEL), lambda b, i, f: (b, 0, 0))
    return pl.pallas_call(
        _ffn_kernel,
        grid=(bsz, seq // tm, nf),
        in_specs=[
            pl.BlockSpec((1, tm, D_MODEL), lambda b, i, f: (b, i, 0)),
            row((1, D_MODEL)), per_b, per_b,
            pl.BlockSpec((D_MODEL, tf), lambda b, i, f: (0, f)),
            pl.BlockSpec((D_MODEL, tf), lambda b, i, f: (0, f + nf)),
            pl.BlockSpec((3, tf), lambda b, i, f: (0, f)),
            pl.BlockSpec((3, tf), lambda b, i, f: (0, f + nf)),
            pl.BlockSpec((tf, D_MODEL), lambda b, i, f: (f, 0)),
            row((1, D_MODEL)), per_b,
        ],
        out_specs=pl.BlockSpec((1, tm, D_MODEL), lambda b, i, f: (b, i, 0)),
        out_shape=jax.ShapeDtypeStruct((bsz, seq, D_MODEL), F32),
        scratch_shapes=[
            pltpu.VMEM((tm, D_MODEL), BF16),
            pltpu.VMEM((tm, D_MODEL), F32),
            pltpu.VMEM((tm + SUBLANES, tf), F32),
            pltpu.VMEM((tm + SUBLANES, tf), F32),
            pltpu.VMEM((nf, SUBLANES, tf), F32),
            pltpu.VMEM((nf, SUBLANES, tf), F32),
        ],
        compiler_params=pltpu.CompilerParams(
            dimension_semantics=("arbitrary", "arbitrary", "arbitrary"),
            vmem_limit_bytes=VMEM_LIMIT),
        name="conv_ffn",
    )(x1, norm_w, scale, shift, w_up, w_up, ffn_conv_w, ffn_conv_w, w_down, post_w, gate)


def _chunk_constants():
    r = lax.broadcasted_iota(jnp.int32, (CHUNK, CHUNK), 0)
    c = lax.broadcasted_iota(jnp.int32, (CHUNK, CHUNK), 1)
    ltri = (c <= r).astype(BF16)
    lvl = jnp.full((CHUNK, CHUNK), -1, jnp.int32)
    for idx, s in enumerate(LEVELS):
        lvl = jnp.where((c < r) & ((r ^ c) >= s) & ((r ^ c) < 2 * s), idx, lvl)
    return ltri, lvl


def kernel(x, c, w_mod, b_mod, mix_pre_w, mix_post_w, w_in, conv_w, gate_w2, gate_b, gla_norm_w,
           w_out, ffn_pre_w, ffn_post_w, w_up, ffn_conv_w, w_down):
    bsz, seq, _ = x.shape
    depth = w_mod.shape[0]
    c_pad = jnp.zeros((SUBLANES, D_MODEL), F32).at[:bsz].set(c)
    ltri, lvl = _chunk_constants()
    for l in range(depth):
        mod = _modulation(c_pad, w_mod[l], b_mod[l][None, :])[:bsz]
        sh_m, sc_m, g_m, sh_f, sc_f, g_f = [m[:, None, :] for m in jnp.split(mod, 6, axis=-1)]

        w_in_pad = jnp.pad(w_in[l].astype(BF16), ((0, 0), (0, D_PROJ - D_IN)))
        gw2_pad = jnp.pad(gate_w2[l].astype(BF16), ((0, LANES - GATE_RANK), (0, 0)))
        proj = _in_proj(x, mix_pre_w[l][None, :], sc_m, sh_m, w_in_pad)
        y_gla = _gla(proj, ltri, lvl, gw2_pad, gate_b[l][None, :], gla_norm_w[l][None, :])
        x = _out_proj(proj, y_gla, x, conv_w[l], w_out[l].astype(BF16), mix_post_w[l][None, :], g_m)
        x = _conv_ffn(x, ffn_pre_w[l][None, :], sc_f, sh_f, w_up[l].astype(BF16), ffn_conv_w[l],
                      w_down[l].astype(BF16), ffn_post_w[l][None, :], g_f)
    return x
```

```python
import functools

import jax
import jax.numpy as jnp
from jax import lax
from jax.experimental import pallas as pl
from jax.experimental.pallas import tpu as pltpu

F32 = jnp.float32
BF16 = jnp.bfloat16

D_MODEL = 2048
CONV_W = 1024
GLA_HEADS = 4
HEAD_K = 128
HEAD_V = 256
GLA_DK = GLA_HEADS * HEAD_K
GLA_DV = GLA_HEADS * HEAD_V
GATE_RANK = 16
GATE_TAU = 16
D_FF = 5632
EPS = 1e-6
D_IN = 3 * CONV_W + 2 * GLA_DK + 2 * GLA_DV + GATE_RANK

LANES = 128
SUBLANES = 8
D_MAIN = D_IN - GATE_RANK
CHUNK = 256
LEVELS = (128, 64, 32, 16, 8)
VMEM_LIMIT = 56 * 1024 * 1024

TM_IN = 1024
TN_IN = 1536
TM_OUT = 512
TM_FFN = 512
TF_FFN = 512


def _rms_scale(x):
    return lax.rsqrt(jnp.mean(x * x, axis=-1, keepdims=True) + EPS)


def _sigmoid(x):
    return 1.0 / (1.0 + jnp.exp(-x))


def _mod_kernel(c_ref, w_ref, b_ref, o_ref):
    c = c_ref[...]
    c_act = c * _sigmoid(c)
    o_ref[...] = jnp.dot(c_act, w_ref[...], preferred_element_type=F32) + b_ref[...]


def _modulation(c_pad, w_mod, b_mod):
    n = w_mod.shape[1]
    tn = 1024
    return pl.pallas_call(
        _mod_kernel,
        grid=(n // tn,),
        in_specs=[
            pl.BlockSpec((SUBLANES, D_MODEL), lambda j: (0, 0)),
            pl.BlockSpec((D_MODEL, tn), lambda j: (0, j)),
            pl.BlockSpec((1, tn), lambda j: (0, j)),
        ],
        out_specs=pl.BlockSpec((SUBLANES, tn), lambda j: (0, j)),
        out_shape=jax.ShapeDtypeStruct((SUBLANES, n), F32),
        compiler_params=pltpu.CompilerParams(
            dimension_semantics=("arbitrary",), vmem_limit_bytes=VMEM_LIMIT),
        name="modulation",
    )(c_pad, w_mod, b_mod)


def _inproj_kernel(x_ref, nw_ref, sc_ref, sh_ref, w_ref, wa_ref, o_ref, alr_ref, h_ref):
    @pl.when(pl.program_id(2) == 0)
    def _():
        x = x_ref[0]
        y = x * _rms_scale(x) * nw_ref[...]
        h_ref[...] = (y * (1.0 + sc_ref[0]) + sh_ref[0]).astype(BF16)
        alr_ref[0] = jnp.dot(h_ref[...], wa_ref[...],
                             preferred_element_type=F32).astype(alr_ref.dtype)

    o_ref[0] = jnp.dot(h_ref[...], w_ref[...], preferred_element_type=F32).astype(o_ref.dtype)


def _in_proj(x, norm_w, scale, shift, w_main, w_gate_pad):
    bsz, seq, _ = x.shape
    tm = min(TM_IN, seq)
    return pl.pallas_call(
        _inproj_kernel,
        grid=(bsz, seq // tm, D_MAIN // TN_IN),
        in_specs=[
            pl.BlockSpec((1, tm, D_MODEL), lambda b, i, j: (b, i, 0)),
            pl.BlockSpec((1, D_MODEL), lambda b, i, j: (0, 0)),
            pl.BlockSpec((1, 1, D_MODEL), lambda b, i, j: (b, 0, 0)),
            pl.BlockSpec((1, 1, D_MODEL), lambda b, i, j: (b, 0, 0)),
            pl.BlockSpec((D_MODEL, TN_IN), lambda b, i, j: (0, j)),
            pl.BlockSpec((D_MODEL, LANES), lambda b, i, j: (0, 0)),
        ],
        out_specs=[
            pl.BlockSpec((1, tm, TN_IN), lambda b, i, j: (b, i, j)),
            pl.BlockSpec((1, tm, LANES), lambda b, i, j: (b, i, 0)),
        ],
        out_shape=[
            jax.ShapeDtypeStruct((bsz, seq, D_MAIN), BF16),
            jax.ShapeDtypeStruct((bsz, seq, LANES), BF16),
        ],
        scratch_shapes=[pltpu.VMEM((tm, D_MODEL), BF16)],
        compiler_params=pltpu.CompilerParams(
            dimension_semantics=("arbitrary", "arbitrary", "arbitrary"),
            vmem_limit_bytes=VMEM_LIMIT),
        name="in_proj",
    )(x, norm_w, scale, shift, w_main, w_gate_pad)


_NT = (((1,), (1,)), ((), ()))
_TN = (((0,), (0,)), ((), ()))
LOG2E = 1.4426950408889634


def _rep_rows(x, row, group):
    n, w = x.shape
    xr = x.reshape(n // group, group, w)
    return jnp.broadcast_to(xr[:, row:row + 1, :], xr.shape).reshape(n, w)


def _rep_rows_ref(ref, row, cols):
    n = ref.shape[0]
    return jnp.concatenate(
        [jnp.broadcast_to(ref[g + row:g + row + 1, cols], (SUBLANES, cols.stop - cols.start))
         for g in range(0, n, SUBLANES)], axis=0)


def _gla_head(h, qk_ref, v_ref, r_ref, b_scr, k_scr, st_scr, lvl, dlane, gnw, o_ref):
    ks = slice(h * HEAD_K, (h + 1) * HEAD_K)
    vs = slice(h * HEAD_V, (h + 1) * HEAD_V)
    half = CHUNK // 2
    b = b_scr[:, ks]
    q = qk_ref[0, :, ks].astype(F32) * (HEAD_K ** -0.5)
    k = qk_ref[0, :, GLA_DK + h * HEAD_K:GLA_DK + (h + 1) * HEAD_K].astype(F32)
    k_scr[...] = k
    v = v_ref[0, :, vs]
    row = lax.broadcasted_iota(jnp.int32, (CHUNK, 1), 0)

    p = jnp.zeros((CHUNK, CHUNK), F32)
    for idx, s in enumerate(LEVELS):
        b_ref = _rep_rows(b, s - 1, 2 * s)
        e = jnp.exp2(jnp.where((row & s) != 0, b - b_ref, b_ref - b))
        out = lax.dot_general((q * e).astype(BF16), (k * e).astype(BF16), _NT,
                              preferred_element_type=F32)
        p = jnp.where(lvl == idx, out, p)

    p_top = p[:half, :half]
    p_bot = p[half:, half:]
    row_in_block = row & (SUBLANES - 1)
    for j in range(SUBLANES):
        d = jnp.where(row_in_block >= j, b - _rep_rows_ref(b_scr, j, ks), -jnp.inf)
        col = jnp.sum(q * _rep_rows_ref(k_scr, j, slice(0, HEAD_K)) * jnp.exp2(d),
                      axis=-1, keepdims=True)
        p_top = jnp.where(dlane == j, col[:half], p_top)
        p_bot = jnp.where(dlane == j, col[half:], p_bot)
    p = jnp.concatenate(
        [jnp.concatenate([p_top, jnp.zeros((half, half), F32)], axis=1),
         jnp.concatenate([p[half:, :half], p_bot], axis=1)], axis=0)

    st = st_scr[h]
    b_last = b[CHUNK - 1:CHUNK, :]
    qe = (q * jnp.exp2(b)).astype(BF16)
    o = jnp.dot(p.astype(BF16), v, preferred_element_type=F32)
    o = o + lax.dot_general(qe, st.astype(BF16), _NT, preferred_element_type=F32)

    ke = (k * jnp.exp2(b_last - b)).astype(BF16)
    st_scr[h] = st * jnp.exp2(b_last) + lax.dot_general(v, ke, _TN, preferred_element_type=F32)

    r = r_ref[0, :, vs].astype(F32)
    y = (o * _rms_scale(o) * gnw) * (r * _sigmoid(r))
    o_ref[0, :, vs] = y.astype(o_ref.dtype)


def _gla_kernel(qk_ref, v_ref, r_ref, alr_ref, ltri_ref, lvl_ref, gw2_ref, gb_ref, gnw_ref,
                o_ref, b_scr, k_scr, st_scr):
    @pl.when(pl.program_id(1) == 0)
    def _():
        st_scr[...] = jnp.zeros(st_scr.shape, F32)

    z = jnp.dot(alr_ref[0], gw2_ref[...], preferred_element_type=F32) + gb_ref[...]
    log_a = (jnp.minimum(z, 0.0) - jnp.log(1.0 + jnp.exp(-jnp.abs(z)))) * (1.0 / GATE_TAU)
    hi = log_a.astype(BF16)
    rem = log_a - hi.astype(F32)
    mid = rem.astype(BF16)
    lo = (rem - mid.astype(F32)).astype(BF16)
    ltri = ltri_ref[...]
    b_scr[...] = LOG2E * (jnp.dot(ltri, hi, preferred_element_type=F32)
                          + jnp.dot(ltri, mid, preferred_element_type=F32)
                          + jnp.dot(ltri, lo, preferred_element_type=F32))

    half = CHUNK // 2
    dlane = (lax.broadcasted_iota(jnp.int32, (half, half), 1)
             - (lax.broadcasted_iota(jnp.int32, (half, half), 0) & ~(SUBLANES - 1)))
    gnw = gnw_ref[...]
    lvl = lvl_ref[...]
    for h in range(GLA_HEADS):
        _gla_head(h, qk_ref, v_ref, r_ref, b_scr, k_scr, st_scr, lvl, dlane, gnw, o_ref)


def _gla(proj, alr, ltri, lvl, gw2_pad, gate_b, gla_norm_w):
    bsz, seq, _ = proj.shape
    assert seq % CHUNK == 0
    wide = lambda col: pl.BlockSpec((1, CHUNK, GLA_DV), lambda b, i, col=col: (b, i, col))
    const = lambda shape: pl.BlockSpec(shape, lambda b, i: (0,) * len(shape))
    return pl.pallas_call(
        _gla_kernel,
        grid=(bsz, seq // CHUNK),
        in_specs=[
            wide(3), wide(4), wide(5),
            pl.BlockSpec((1, CHUNK, LANES), lambda b, i: (b, i, 0)),
            const((CHUNK, CHUNK)), const((CHUNK, CHUNK)), const((LANES, GLA_DK)),
            const((1, GLA_DK)), const((1, HEAD_V)),
        ],
        out_specs=pl.BlockSpec((1, CHUNK, GLA_DV), lambda b, i: (b, i, 0)),
        out_shape=jax.ShapeDtypeStruct((bsz, seq, GLA_DV), BF16),
        scratch_shapes=[
            pltpu.VMEM((CHUNK, GLA_DK), F32),
            pltpu.VMEM((CHUNK, HEAD_K), F32),
            pltpu.VMEM((GLA_HEADS, HEAD_V, HEAD_K), F32),
        ],
        compiler_params=pltpu.CompilerParams(
            dimension_semantics=("arbitrary", "arbitrary"), vmem_limit_bytes=VMEM_LIMIT),
        name="gla",
    )(proj, proj, proj, alr, ltri, lvl, gw2_pad, gate_b, gla_norm_w)


def _outproj_kernel(cb_ref, cc_ref, cx_ref, yg_ref, x_ref, cw_ref, wc_ref, wg_ref, nw_ref, g_ref,
                    o_ref, u_scr):
    tm = cb_ref.shape[1]

    @pl.when(pl.program_id(1) == 0)
    def _():
        u_scr[0:SUBLANES, :] = jnp.zeros((SUBLANES, CONV_W), F32)

    @pl.when(pl.program_id(1) > 0)
    def _():
        u_scr[0:SUBLANES, :] = u_scr[tm:tm + SUBLANES, :]

    u = cc_ref[0].astype(F32) * cx_ref[0].astype(F32)
    u_scr[SUBLANES:tm + SUBLANES, :] = u
    cw = cw_ref[...]
    conv = (u_scr[SUBLANES - 2:tm + SUBLANES - 2, :] * cw[0:1]
            + u_scr[SUBLANES - 1:tm + SUBLANES - 1, :] * cw[1:2]
            + u * cw[2:3])
    y_conv = (cb_ref[0].astype(F32) * conv).astype(BF16)

    y = (jnp.dot(y_conv, wc_ref[...], preferred_element_type=F32)
         + jnp.dot(yg_ref[0], wg_ref[...], preferred_element_type=F32))
    o_ref[0] = x_ref[0] + g_ref[0] * (y * _rms_scale(y) * nw_ref[...])


def _out_proj(proj, y_gla, x, conv_w, w_out, norm_w, gate):
    bsz, seq, _ = x.shape
    tm = min(TM_OUT, seq)
    wide = lambda col: pl.BlockSpec((1, tm, CONV_W), lambda b, i, col=col: (b, i, col))
    const = lambda shape: pl.BlockSpec(shape, lambda b, i: (0,) * len(shape))
    return pl.pallas_call(
        _outproj_kernel,
        grid=(bsz, seq // tm),
        in_specs=[
            wide(0), wide(1), wide(2),
            pl.BlockSpec((1, tm, GLA_DV), lambda b, i: (b, i, 0)),
            pl.BlockSpec((1, tm, D_MODEL), lambda b, i: (b, i, 0)),
            const((3, CONV_W)),
            pl.BlockSpec((CONV_W, D_MODEL), lambda b, i: (0, 0)),
            pl.BlockSpec((GLA_DV, D_MODEL), lambda b, i: (1, 0)),
            const((1, D_MODEL)),
            pl.BlockSpec((1, 1, D_MODEL), lambda b, i: (b, 0, 0)),
        ],
        out_specs=pl.BlockSpec((1, tm, D_MODEL), lambda b, i: (b, i, 0)),
        out_shape=jax.ShapeDtypeStruct((bsz, seq, D_MODEL), F32),
        scratch_shapes=[pltpu.VMEM((tm + SUBLANES, CONV_W), F32)],
        compiler_params=pltpu.CompilerParams(
            dimension_semantics=("arbitrary", "arbitrary"), vmem_limit_bytes=VMEM_LIMIT),
        name="out_proj",
    )(proj, proj, proj, y_gla, x, conv_w, w_out, w_out, norm_w, gate)


def _ffn_kernel(x_ref, xn_ref, nw_ref, sc_ref, sh_ref, wg_ref, wv_ref, cwg_ref, cwv_ref, wd_ref,
                pw_ref, g_ref, o_ref, h_scr, acc_scr, ug_scr, uv_scr, act0_scr, act1_scr, hg_scr, hv_scr,
                *, nf, nt):
    s = pl.program_id(0)
    tm = x_ref.shape[1]
    fa = s % nf
    lo = SUBLANES

    @pl.when(s == 0)
    def _():
        acc_scr[...] = jnp.zeros(acc_scr.shape, F32)
        act0_scr[...] = jnp.zeros(act0_scr.shape, BF16)
        act1_scr[...] = jnp.zeros(act1_scr.shape, BF16)
        hg_scr[...] = jnp.zeros(hg_scr.shape, F32)
        hv_scr[...] = jnp.zeros(hv_scr.shape, F32)

    @pl.when(fa == 0)
    def _():
        x = xn_ref[0]
        y = x * _rms_scale(x) * nw_ref[...]
        h_scr[...] = (y * (1.0 + sc_ref[0]) + sh_ref[0]).astype(BF16)

    def up_conv(w_ref, cw_ref, u_scr, hist):
        u = jnp.dot(h_scr[...], w_ref[...], preferred_element_type=F32)
        u_scr[0:lo, :] = jnp.where((s // nf) % nt == 0, 0.0, hist[fa])
        u_scr[lo:tm + lo, :] = u
        hist[fa] = u[tm - lo:, :]
        cw = cw_ref[...]
        return (u_scr[lo - 2:tm + lo - 2, :] * cw[0:1] + u_scr[lo - 1:tm + lo - 1, :] * cw[1:2]
                + u * cw[2:3])

    def stages(act_a, act_b):
        cg = up_conv(wg_ref, cwg_ref, ug_scr, hg_scr)
        cv = up_conv(wv_ref, cwv_ref, uv_scr, hv_scr)
        act_a[...] = (cg * _sigmoid(cg) * cv).astype(BF16)
        acc_scr[...] += jnp.dot(act_b[...], wd_ref[...], preferred_element_type=F32)

    pl.when(s % 2 == 0)(lambda: stages(act0_scr, act1_scr))
    pl.when(s % 2 == 1)(lambda: stages(act1_scr, act0_scr))

    @pl.when(jnp.logical_and(fa == 0, s > 0))
    def _():
        y = acc_scr[...]
        o_ref[0] = x_ref[0] + g_ref[0] * (y * _rms_scale(y) * pw_ref[...])
        acc_scr[...] = jnp.zeros(acc_scr.shape, F32)


def _conv_ffn(x1, norm_w, scale, shift, w_up, ffn_conv_w, w_down, post_w, gate):
    bsz, seq, _ = x1.shape
    tm = min(TM_FFN, seq)
    tf = TF_FFN
    nf = D_FF // tf
    nt = seq // tm
    n_tiles = bsz * nt
    tile_a = lambda s: jnp.minimum(s // nf, n_tiles - 1)
    tile_b = lambda s: jnp.maximum(s - 1, 0) // nf
    fb = lambda s: (s + nf - 1) % nf
    row_a = lambda s: (tile_a(s) // nt, tile_a(s) % nt, 0)
    row_b = lambda s: (tile_b(s) // nt, tile_b(s) % nt, 0)
    vec = pl.BlockSpec((1, D_MODEL), lambda s: (0, 0))
    return pl.pallas_call(
        functools.partial(_ffn_kernel, nf=nf, nt=nt),
        grid=(n_tiles * nf + 1,),
        in_specs=[
            pl.BlockSpec((1, tm, D_MODEL), row_b),
            pl.BlockSpec((1, tm, D_MODEL), row_a),
            vec,
            pl.BlockSpec((1, 1, D_MODEL), lambda s: (tile_a(s) // nt, 0, 0)),
            pl.BlockSpec((1, 1, D_MODEL), lambda s: (tile_a(s) // nt, 0, 0)),
            pl.BlockSpec((D_MODEL, tf), lambda s: (0, s % nf)),
            pl.BlockSpec((D_MODEL, tf), lambda s: (0, s % nf + nf)),
            pl.BlockSpec((3, tf), lambda s: (0, s % nf)),
            pl.BlockSpec((3, tf), lambda s: (0, s % nf + nf)),
            pl.BlockSpec((tf, D_MODEL), lambda s: (fb(s), 0)),
            vec,
            pl.BlockSpec((1, 1, D_MODEL), lambda s: (tile_b(s) // nt, 0, 0)),
        ],
        out_specs=pl.BlockSpec((1, tm, D_MODEL), row_b),
        out_shape=jax.ShapeDtypeStruct((bsz, seq, D_MODEL), F32),
        scratch_shapes=[
            pltpu.VMEM((tm, D_MODEL), BF16),
            pltpu.VMEM((tm, D_MODEL), F32),
            pltpu.VMEM((tm + SUBLANES, tf), F32),
            pltpu.VMEM((tm + SUBLANES, tf), F32),
            pltpu.VMEM((tm, tf), BF16),
            pltpu.VMEM((tm, tf), BF16),
            pltpu.VMEM((nf, SUBLANES, tf), F32),
            pltpu.VMEM((nf, SUBLANES, tf), F32),
        ],
        compiler_params=pltpu.CompilerParams(
            dimension_semantics=("arbitrary",), vmem_limit_bytes=VMEM_LIMIT),
        name="conv_ffn",
    )(x1, x1, norm_w, scale, shift, w_up, w_up, ffn_conv_w, ffn_conv_w, w_down, post_w, gate)


def _chunk_constants():
    r = lax.broadcasted_iota(jnp.int32, (CHUNK, CHUNK), 0)
    c = lax.broadcasted_iota(jnp.int32, (CHUNK, CHUNK), 1)
    ltri = (c <= r).astype(BF16)
    lvl = jnp.full((CHUNK, CHUNK), -1, jnp.int32)
    for idx, s in enumerate(LEVELS):
        lvl = jnp.where((c < r) & ((r ^ c) >= s) & ((r ^ c) < 2 * s), idx, lvl)
    return ltri, lvl


def kernel(x, c, w_mod, b_mod, mix_pre_w, mix_post_w, w_in, conv_w, gate_w2, gate_b, gla_norm_w,
           w_out, ffn_pre_w, ffn_post_w, w_up, ffn_conv_w, w_down):
    bsz, seq, _ = x.shape
    depth = w_mod.shape[0]
    c_pad = jnp.zeros((SUBLANES, D_MODEL), F32).at[:bsz].set(c)
    ltri, lvl = _chunk_constants()
    for l in range(depth):
        mod = _modulation(c_pad, w_mod[l], b_mod[l][None, :])[:bsz]
        sh_m, sc_m, g_m, sh_f, sc_f, g_f = [m[:, None, :] for m in jnp.split(mod, 6, axis=-1)]

        w_in_bf = w_in[l].astype(BF16)
        w_gate_pad = jnp.pad(w_in_bf[:, D_MAIN:], ((0, 0), (0, LANES - GATE_RANK)))
        gw2_pad = jnp.pad(gate_w2[l].astype(BF16), ((0, LANES - GATE_RANK), (0, 0)))
        proj, alr = _in_proj(x, mix_pre_w[l][None, :], sc_m, sh_m, w_in_bf, w_gate_pad)
        y_gla = _gla(proj, alr, ltri, lvl, gw2_pad, gate_b[l][None, :], gla_norm_w[l][None, :])
        x = _out_proj(proj, y_gla, x, conv_w[l], w_out[l].astype(BF16), mix_post_w[l][None, :], g_m)
        x = _conv_ffn(x, ffn_pre_w[l][None, :], sc_f, sh_f, w_up[l].astype(BF16), ffn_conv_w[l],
                      w_down[l].astype(BF16), ffn_post_w[l][None, :], g_f)
    return x
```
